```python
import math
import jax, jax.numpy as jnp
from jax import lax
import numpy as np

D_MODEL = 1024
BATCH = 2
SEQ = 8192
DEPTH = 1
DEC_BATCH = 128
DEC_SEQ = 4
PAST_LEN = 8192
PAGE_SIZE = 128

MIX_WIDTH = 2 * D_MODEL
HEAD_DIM = 64
ATT_WIDTH = MIX_WIDTH // 4
ATT_HEADS = ATT_WIDTH // HEAD_DIM
SSD_WIDTH = MIX_WIDTH - ATT_WIDTH
SSD_HEADS = SSD_WIDTH // HEAD_DIM
SSD_GROUPS = 4
SSD_STATE = 128
SSD_CONV = 4
SSD_CHUNK = 128
CONV_DIM = SSD_WIDTH + 2 * SSD_GROUPS * SSD_STATE
IN_WIDTH = SSD_WIDTH + CONV_DIM + SSD_HEADS + 3 * ATT_WIDTH
DIL_PATTERNS = ((128, 1), (512, 4), (2048, 16))
W_MAX = max(w for w, _ in DIL_PATTERNS)
ATT_BLOCK = 128
ROPE_THETA = 10000.0
PEER_NKEYS = 128
PEER_EXPERTS = PEER_NKEYS * PEER_NKEYS
PEER_HEADS = 8
PEER_TOPK = 16
PEER_DKEY = 256
PEER_BLOCK = 128
ALPHA = (2 * DEPTH) ** 0.25
BETA = (8 * DEPTH) ** -0.25
LN_EPS = 1e-5

kernel_name = 'hymba_ssd_dilated_window_peer_step'

F32 = jnp.float32


def layer_norm(x, g, b):
    xf = x.astype(F32)
    mu = jnp.mean(xf, axis=-1, keepdims=True)
    xc = xf - mu
    var = jnp.mean(xc * xc, axis=-1, keepdims=True)
    return (xc * lax.rsqrt(var + LN_EPS) * g.astype(F32) + b.astype(F32)).astype(x.dtype)


def rope(t, pos):
    half = HEAD_DIM // 2
    inv = ROPE_THETA ** (-jnp.arange(half, dtype=F32) / half)
    ang = pos.astype(F32)[:, None] * inv[None, :]
    cos = jnp.cos(ang)[None, :, None, :]
    sin = jnp.sin(ang)[None, :, None, :]
    tf = t.astype(F32)
    t1, t2 = tf[..., :half], tf[..., half:]
    return jnp.concatenate([t1 * cos - t2 * sin, t2 * cos + t1 * sin], axis=-1).astype(t.dtype)


def mixer_inputs(x, pos, w_in):
    b, L, _ = x.shape
    proj = jnp.einsum('bld,de->ble', x, w_in)
    o1 = SSD_WIDTH
    o2 = o1 + CONV_DIM
    o3 = o2 + SSD_HEADS
    z = proj[..., :o1]
    xbc = proj[..., o1:o2]
    dt_raw = proj[..., o2:o3]
    qkv = proj[..., o3:].reshape(b, L, 3, ATT_HEADS, HEAD_DIM)
    q = rope(qkv[:, :, 0], pos)
    k = rope(qkv[:, :, 1], pos)
    v = qkv[:, :, 2]
    return z, xbc, dt_raw, q, k, v


def ssd_scan(x, dt, A, Bm, Cm, h0):
    b, L, H, P = x.shape
    G, N = Bm.shape[2], Bm.shape[3]
    R = H // G
    q = min(SSD_CHUNK, L)
    pad = (-L) % q
    if pad:
        pw = lambda t: jnp.pad(t, [(0, 0), (0, pad)] + [(0, 0)] * (t.ndim - 2))
        x, dt, Bm, Cm = pw(x), pw(dt), pw(Bm), pw(Cm)
    c = (L + pad) // q
    x = x.reshape(b, c, q, G, R, P)
    dt = dt.reshape(b, c, q, G, R)
    Bm = Bm.reshape(b, c, q, G, N)
    Cm = Cm.reshape(b, c, q, G, N)
    acs = jnp.cumsum(dt * A.reshape(G, R), axis=2)
    xdt = x * dt[..., None]
    causal = jnp.tril(jnp.ones((q, q), dtype=bool))[:, :, None, None]
    seg = acs[:, :, :, None] - acs[:, :, None, :]
    decay = jnp.exp(jnp.where(causal, seg, -jnp.inf))
    cb = jnp.einsum('bclgn,bcsgn->bclsg', Cm, Bm)
    y_diag = jnp.einsum('bclsgr,bcsgrp->bclgrp', cb[..., None] * decay, xdt)
    decay_to_end = jnp.exp(acs[:, :, -1:] - acs)
    states = jnp.einsum('bclgn,bclgrp->bcgrpn', Bm, xdt * decay_to_end[..., None])
    chunk_decay = jnp.exp(acs[:, :, -1])

    def step(h, inp):
        dec, st = inp
        return dec[..., None, None] * h + st, h

    h_final, h_in = lax.scan(step, h0.reshape(b, G, R, P, N),
                             (jnp.moveaxis(chunk_decay, 1, 0), jnp.moveaxis(states, 1, 0)))
    h_in = jnp.moveaxis(h_in, 0, 1)
    y_off = jnp.einsum('bclgn,bcgrpn->bclgrp', Cm, h_in) * jnp.exp(acs)[..., None]
    y = (y_diag + y_off).reshape(b, c * q, H, P)[:, :L]
    return y, h_final.reshape(b, H, P, N)


def ssd_mixer(z, xbc, dt_raw, conv_prev, h0, conv_w, conv_b, dt_bias, a_log, d_skip, norm_g):
    b, L, _ = z.shape
    xbc_in = jnp.concatenate([conv_prev.astype(xbc.dtype), xbc], axis=1)
    conv = conv_b + xbc_in[:, 0:L] * conv_w[0]
    for tap in range(1, SSD_CONV):
        conv = conv + xbc_in[:, tap:tap + L] * conv_w[tap]
    xbc_act = jax.nn.silu(conv)
    new_conv = xbc_in[:, -(SSD_CONV - 1):]
    gn = SSD_GROUPS * SSD_STATE
    xs = xbc_act[..., :SSD_WIDTH].reshape(b, L, SSD_HEADS, HEAD_DIM).astype(F32)
    bm = xbc_act[..., SSD_WIDTH:SSD_WIDTH + gn].reshape(b, L, SSD_GROUPS, SSD_STATE).astype(F32)
    cm = xbc_act[..., SSD_WIDTH + gn:].reshape(b, L, SSD_GROUPS, SSD_STATE).astype(F32)
    dt = jax.nn.softplus(dt_raw.astype(F32) + dt_bias.astype(F32))
    A = -jnp.exp(a_log.astype(F32))
    y, h = ssd_scan(xs, dt, A, bm, cm, h0.astype(F32))
    y = y + xs * d_skip.astype(F32)[:, None]
    gated = y.reshape(b, L, SSD_WIDTH) * jax.nn.silu(z.astype(F32))
    gg = gated.reshape(b, L, SSD_GROUPS, SSD_WIDTH // SSD_GROUPS)
    gg = gg * lax.rsqrt(jnp.mean(gg * gg, axis=-1, keepdims=True) + LN_EPS)
    out = (gg.reshape(b, L, SSD_WIDTH) * norm_g.astype(F32)).astype(z.dtype)
    return out, new_conv, h.astype(h0.dtype)


def band_attention(q, k, v, n):
    b, L, H, hd = q.shape
    nb = -(-L // ATT_BLOCK)
    Lp = nb * ATT_BLOCK
    if Lp > L:
        pw = lambda t: jnp.pad(t, [(0, 0), (0, Lp - L), (0, 0), (0, 0)])
        q, k, v = pw(q), pw(k), pw(v)
    qb = q.reshape(b, nb, ATT_BLOCK, H, hd)
    kb = k.reshape(b, nb, ATT_BLOCK, H, hd)
    vb = v.reshape(b, nb, ATT_BLOCK, H, hd)
    kc = jnp.concatenate([jnp.concatenate([jnp.zeros_like(kb[:, :1]), kb[:, :-1]], axis=1), kb], axis=2)
    vc = jnp.concatenate([jnp.concatenate([jnp.zeros_like(vb[:, :1]), vb[:, :-1]], axis=1), vb], axis=2)
    s = jnp.einsum('bnqhd,bnkhd->bnhqk', qb, kc).astype(F32) * (hd ** -0.5)
    qi = jnp.arange(ATT_BLOCK)[:, None] + ATT_BLOCK
    ki = jnp.arange(2 * ATT_BLOCK)[None, :]
    dist = qi - ki
    key_pos = (jnp.arange(nb) * ATT_BLOCK)[:, None, None] + ki[None] - ATT_BLOCK
    valid = (dist >= 0)[None] & (dist <= n)[None] & (key_pos >= 0)
    s = jnp.where(valid[None, :, None], s, -jnp.inf)
    lse = jax.nn.logsumexp(s, axis=-1)
    p = jnp.exp(s - lse[..., None])
    o = jnp.einsum('bnhqk,bnkhd->bnqhd', p.astype(v.dtype), vc)
    o = o.reshape(b, Lp, H, hd)[:, :L]
    lse = jnp.transpose(lse, (0, 1, 3, 2)).reshape(b, Lp, H)[:, :L]
    return o, lse


def combine_dilations(outs, lses, dtype):
    wts = jax.nn.softmax(jnp.stack(lses), axis=0)
    o = jnp.einsum('pblh,pblhd->blhd', wts, jnp.stack(outs).astype(F32))
    return o.astype(dtype)


def dilated_attention_prompt(q, k, v):
    b, L, H, hd = q.shape
    outs, lses = [], []
    for w, d in DIL_PATTERNS:
        n = w // d
        to_res = lambda t: jnp.transpose(t.reshape(b, L // d, d, H, hd), (0, 2, 1, 3, 4)).reshape(b * d, L // d, H, hd)
        o, lse = band_attention(to_res(q), to_res(k), to_res(v), n)
        o = jnp.transpose(o.reshape(b, d, L // d, H, hd), (0, 2, 1, 3, 4)).reshape(b, L, H, hd)
        lse = jnp.transpose(lse.reshape(b, d, L // d, H), (0, 2, 1, 3)).reshape(b, L, H)
        outs.append(o)
        lses.append(lse)
    return combine_dilations(outs, lses, q.dtype)


def dilated_attention_sample(q, k_all, v_all):
    b, Lq = q.shape[0], q.shape[1]
    hd = q.shape[-1]
    off = k_all.shape[1] - Lq
    outs, lses = [], []
    for w, d in DIL_PATTERNS:
        n = w // d
        idx = off + jnp.arange(Lq)[:, None] - d * jnp.arange(n + 1)[None, :]
        valid = idx >= 0
        idx = jnp.maximum(idx, 0)
        kg = k_all[:, idx]
        vg = v_all[:, idx]
        s = jnp.einsum('bqhd,bqmhd->bhqm', q, kg).astype(F32) * (hd ** -0.5)
        s = jnp.where(valid[None, None], s, -jnp.inf)
        lse = jax.nn.logsumexp(s, axis=-1)
        p = jnp.exp(s - lse[..., None])
        o = jnp.einsum('bhqm,bqmhd->bqhd', p.astype(v_all.dtype), vg)
        outs.append(o)
        lses.append(jnp.transpose(lse, (0, 2, 1)))
    return combine_dilations(outs, lses, q.dtype)


def peer_ffn(x, w_q, keys_1, keys_2, u_tab, v_tab):
    shp = x.shape
    xt = x.reshape(-1, D_MODEL)
    T = xt.shape[0]
    nblk = -(-T // PEER_BLOCK)
    xp = jnp.pad(xt, [(0, nblk * PEER_BLOCK - T), (0, 0)])
    half = PEER_DKEY // 2

    def block(xb):
        qh = jnp.einsum('td,de->te', xb, w_q).reshape(-1, PEER_HEADS, PEER_DKEY)
        s1 = jnp.einsum('thc,kc->thk', qh[..., :half], keys_1)
        s2 = jnp.einsum('thc,kc->thk', qh[..., half:], keys_2)
        v1, i1 = lax.top_k(s1, PEER_TOPK)
        v2, i2 = lax.top_k(s2, PEER_TOPK)
        cand = (v1[..., :, None] + v2[..., None, :]).reshape(-1, PEER_HEADS, PEER_TOPK * PEER_TOPK)
        sc, ci = lax.top_k(cand, PEER_TOPK)
        e = (jnp.take_along_axis(i1, ci // PEER_TOPK, axis=-1) * PEER_NKEYS
             + jnp.take_along_axis(i2, ci % PEER_TOPK, axis=-1))
        g = jax.nn.softmax(sc.astype(F32), axis=-1)
        hid = jax.nn.gelu(jnp.einsum('thkd,td->thk', u_tab[e], xb).astype(F32), approximate=False)
        return jnp.einsum('thk,thkd->td', (g * hid).astype(xb.dtype), v_tab[e])

    out = lax.map(block, xp.reshape(nblk, PEER_BLOCK, D_MODEL))
    return out.reshape(-1, D_MODEL)[:T].reshape(shp)


def finish_layer(x, y_ssd, att, w_out, ln1_g, ln1_b, peer_w_q, peer_keys_1, peer_keys_2,
                 peer_u, peer_v, ln2_g, ln2_b):
    b, L, _ = x.shape
    mix = jnp.einsum('ble,ed->bld', jnp.concatenate([y_ssd, att.reshape(b, L, ATT_WIDTH)], axis=-1), w_out)
    h = layer_norm(ALPHA * x + mix, ln1_g, ln1_b)
    return layer_norm(ALPHA * h + peer_ffn(h, peer_w_q, peer_keys_1, peer_keys_2, peer_u, peer_v), ln2_g, ln2_b)


def setup_inputs(seed: int = 0) -> dict:
    key = jax.random.key(seed)
    ks = jax.random.split(key, 24)
    buf = min(W_MAX, PAST_LEN)
    nrm = lambda k, shape, s: jax.random.normal(k, shape, F32) * s
    dt0 = jnp.exp(jax.random.uniform(ks[9], (DEPTH, SSD_HEADS), F32) * (math.log(0.1) - math.log(0.001)) + math.log(0.001))
    return {
        'x_prompt': nrm(ks[0], (BATCH, SEQ, D_MODEL), 1.0),
        'x_sample': nrm(ks[1], (DEC_BATCH, DEC_SEQ, D_MODEL), 1.0),
        'cache_attn_k': nrm(ks[2], (DEPTH, DEC_BATCH, buf, ATT_HEADS, HEAD_DIM), 1.0),
        'cache_attn_v': nrm(ks[3], (DEPTH, DEC_BATCH, buf, ATT_HEADS, HEAD_DIM), 1.0),
        'state_conv': nrm(ks[4], (DEPTH, DEC_BATCH, SSD_CONV - 1, CONV_DIM), 1.0),
        'state_ssm': nrm(ks[5], (DEPTH, DEC_BATCH, SSD_HEADS, HEAD_DIM, SSD_STATE), 0.5),
        'w_in': nrm(ks[6], (DEPTH, D_MODEL, IN_WIDTH), D_MODEL ** -0.5),
        'conv_w': nrm(ks[7], (DEPTH, SSD_CONV, CONV_DIM), SSD_CONV ** -0.5),
        'conv_b': nrm(ks[8], (DEPTH, CONV_DIM), 0.02),
        'dt_bias': dt0 + jnp.log(-jnp.expm1(-dt0)),
        'a_log': jnp.log(jax.random.uniform(ks[10], (DEPTH, SSD_HEADS), F32, 1.0, 16.0)),
        'd_skip': 1.0 + nrm(ks[11], (DEPTH, SSD_HEADS), 0.1),
        'ssd_norm_g': 1.0 + nrm(ks[12], (DEPTH, SSD_WIDTH), 0.05),
        'w_out': nrm(ks[13], (DEPTH, MIX_WIDTH, D_MODEL), BETA * MIX_WIDTH ** -0.5),
        'ln1_g': 1.0 + nrm(ks[14], (DEPTH, D_MODEL), 0.05),
        'ln1_b': nrm(ks[15], (DEPTH, D_MODEL), 0.02),
        'peer_w_q': nrm(ks[16], (DEPTH, D_MODEL, PEER_HEADS * PEER_DKEY), D_MODEL ** -0.5),
        'peer_keys_1': nrm(ks[17], (DEPTH, PEER_NKEYS, PEER_DKEY // 2), (PEER_DKEY // 2) ** -0.5),
        'peer_keys_2': nrm(ks[18], (DEPTH, PEER_NKEYS, PEER_DKEY // 2), (PEER_DKEY // 2) ** -0.5),
        'peer_u': nrm(ks[19], (DEPTH, PEER_EXPERTS, D_MODEL), D_MODEL ** -0.5),
        'peer_v': nrm(ks[20], (DEPTH, PEER_EXPERTS, D_MODEL), BETA * PEER_HEADS ** -0.5),
        'ln2_g': 1.0 + nrm(ks[21], (DEPTH, D_MODEL), 0.05),
        'ln2_b': nrm(ks[22], (DEPTH, D_MODEL), 0.02),
    }


def reference(x_prompt, x_sample, cache_attn_k, cache_attn_v, state_conv, state_ssm,
              w_in, conv_w, conv_b, dt_bias, a_log, d_skip, ssd_norm_g, w_out, ln1_g, ln1_b,
              peer_w_q, peer_keys_1, peer_keys_2, peer_u, peer_v, ln2_g, ln2_b):
    bp, Lp_len = x_prompt.shape[0], x_prompt.shape[1]
    Ls = x_sample.shape[1]
    pos_p = jnp.arange(Lp_len)
    pos_s = PAST_LEN + jnp.arange(Ls)
    buf_p = min(W_MAX, Lp_len)
    h_p, h_s = x_prompt, x_sample
    pk, pv, pc, ps, sk, sv, sc, ss = [], [], [], [], [], [], [], []
    for l in range(DEPTH):
        ssd_w = (conv_w[l], conv_b[l], dt_bias[l], a_log[l], d_skip[l], ssd_norm_g[l])
        tail_w = (w_out[l], ln1_g[l], ln1_b[l], peer_w_q[l], peer_keys_1[l], peer_keys_2[l],
                  peer_u[l], peer_v[l], ln2_g[l], ln2_b[l])
        z, xbc, dtr, q, k, v = mixer_inputs(h_p, pos_p, w_in[l])
        conv0 = jnp.zeros((bp, SSD_CONV - 1, CONV_DIM), h_p.dtype)
        ssm0 = jnp.zeros((bp, SSD_HEADS, HEAD_DIM, SSD_STATE), h_p.dtype)
        y_ssd, conv_new, ssm_new = ssd_mixer(z, xbc, dtr, conv0, ssm0, *ssd_w)
        att = dilated_attention_prompt(q, k, v)
        pk.append(k[:, -buf_p:])
        pv.append(v[:, -buf_p:])
        pc.append(conv_new)
        ps.append(ssm_new)
        h_p = finish_layer(h_p, y_ssd, att, *tail_w)
        z, xbc, dtr, q, k, v = mixer_inputs(h_s, pos_s, w_in[l])
        y_ssd, conv_new, ssm_new = ssd_mixer(z, xbc, dtr, state_conv[l], state_ssm[l], *ssd_w)
        k_all = jnp.concatenate([cache_attn_k[l].astype(k.dtype), k], axis=1)
        v_all = jnp.concatenate([cache_attn_v[l].astype(v.dtype), v], axis=1)
        att = dilated_attention_sample(q, k_all, v_all)
        buf_s = cache_attn_k.shape[2]
        sk.append(k_all[:, -buf_s:])
        sv.append(v_all[:, -buf_s:])
        sc.append(conv_new)
        ss.append(ssm_new)
        h_s = finish_layer(h_s, y_ssd, att, *tail_w)
    prompt_attn_k, prompt_attn_v = jnp.stack(pk), jnp.stack(pv)
    prompt_conv, prompt_ssm = jnp.stack(pc), jnp.stack(ps)
    sample_attn_k, sample_attn_v = jnp.stack(sk), jnp.stack(sv)
    sample_conv, sample_ssm = jnp.stack(sc), jnp.stack(ss)
    return (h_p, h_s, prompt_attn_k, prompt_attn_v, prompt_conv, prompt_ssm,
            sample_attn_k, sample_attn_v, sample_conv, sample_ssm)
```

```python
import functools
import math

import jax
import jax.numpy as jnp
from jax import lax
from jax.experimental import pallas as pl
from jax.experimental.pallas import tpu as pltpu

F32 = jnp.float32
BF16 = jnp.bfloat16

LANES = 128
SUBLANES = 8
VMEM_LIMIT_BYTES = 56 * 1024 * 1024

D_MODEL = 1024
HEAD_DIM = 64
ATT_HEADS = 8
ATT_WIDTH = ATT_HEADS * HEAD_DIM
SSD_HEADS = 24
SSD_WIDTH = SSD_HEADS * HEAD_DIM
SSD_GROUPS = 4
SSD_STATE = 128
SSD_CONV = 4
SSD_CHUNK = 128
CONV_DIM = SSD_WIDTH + 2 * SSD_GROUPS * SSD_STATE
DIL_PATTERNS = ((128, 1), (512, 4), (2048, 16))
W_MAX = 2048
PAST_LEN = 8192
ROPE_THETA = 10000.0
PEER_NKEYS = 128
PEER_HEADS = 8
PEER_TOPK = 16
PEER_DKEY = 256
ALPHA = 2.0 ** 0.25
LN_EPS = 1e-5

NEG_INF = float("-inf")


def _cparams(*sem):
    return pltpu.CompilerParams(dimension_semantics=sem, vmem_limit_bytes=VMEM_LIMIT_BYTES)


def _compare_exchange(v, i, j):
    hi = jnp.maximum(v[i], v[j])
    lo = jnp.minimum(v[i], v[j])
    v[i], v[j] = hi, lo


def _bitonic_sort_desc(v):
    n = len(v)
    k = 2
    while k <= n:
        j = k // 2
        while j >= 1:
            for i in range(n):
                l = i ^ j
                if l > i:
                    if (i & k) == 0:
                        _compare_exchange(v, i, l)
                    else:
                        _compare_exchange(v, l, i)
            j //= 2
        k *= 2
    return v


def _bitonic_merge_desc(v):
    n = len(v)
    j = n // 2
    while j >= 1:
        for i in range(n):
            if (i & j) == 0:
                _compare_exchange(v, i, i + j)
        j //= 2
    return v


def _top_sorted_union(a, b):
    n = len(a)
    return _bitonic_merge_desc([jnp.maximum(a[i], b[n - 1 - i]) for i in range(n)])


def _top16_over_keys(s):
    t = s.shape[1]
    v = [s[SUBLANES * i:SUBLANES * (i + 1), :] for i in range(PEER_NKEYS // SUBLANES)]
    v = _bitonic_sort_desc(v)
    shift = SUBLANES // 2
    while shift >= 1:
        w = [pltpu.roll(x, SUBLANES - shift, 0) for x in v]
        v = _top_sorted_union(v, w)
        shift //= 2
    del t
    return v


def _peer_route_kernel(h_ref, wq_ref, k1_ref, k2_ref, s1_ref, s2_ref, st_ref):
    tt = h_ref.shape[0]
    half = PEER_DKEY // 2
    qh = jnp.dot(h_ref[...], wq_ref[...], preferred_element_type=F32).astype(BF16)
    row = lax.broadcasted_iota(jnp.int32, (SUBLANES, tt), 0)
    v1 = [jnp.zeros((SUBLANES, tt), F32) for _ in range(PEER_TOPK)]
    v2 = [jnp.zeros((SUBLANES, tt), F32) for _ in range(PEER_TOPK)]
    nt = (((1,), (1,)), ((), ()))
    for h in range(PEER_HEADS):
        q1 = qh[:, h * PEER_DKEY:h * PEER_DKEY + half]
        q2 = qh[:, h * PEER_DKEY + half:(h + 1) * PEER_DKEY]
        s1 = lax.dot_general(k1_ref[...], q1, nt, preferred_element_type=F32)
        s2 = lax.dot_general(k2_ref[...], q2, nt, preferred_element_type=F32)
        s1_ref[h] = s1
        s2_ref[h] = s2
        t1 = _top16_over_keys(s1)
        t2 = _top16_over_keys(s2)
        for a in range(PEER_TOPK):
            r1 = pltpu.roll(t1[a], h, 0) if h else t1[a]
            r2 = pltpu.roll(t2[a], h, 0) if h else t2[a]
            v1[a] = jnp.where(row == h, r1, v1[a])
            v2[a] = jnp.where(row == h, r2, v2[a])
    top = [v1[0] + v2[b] for b in range(PEER_TOPK)]
    neg = jnp.full((SUBLANES, tt), NEG_INF, F32)
    for a in range(1, PEER_TOPK):
        n_a = PEER_TOPK // (a + 1)
        r = [v1[a] + v2[b] for b in range(n_a)] + [neg] * (PEER_TOPK - n_a)
        top = _top_sorted_union(top, r)
    m = top[0]
    z = jnp.zeros_like(m)
    for i in range(PEER_TOPK):
        z = z + jnp.exp(top[i] - m)
    st_ref[0] = top[PEER_TOPK - 1]
    st_ref[1] = 1.0 / z
    st_ref[2] = v1[0]
    st_ref[3] = v2[0]


def _peer_route(h_bf, wq_bf, k1_bf, k2_bf, tile):
    t = h_bf.shape[0]
    grid = (t // tile,)
    return pl.pallas_call(
        _peer_route_kernel,
        grid=grid,
        in_specs=[
            pl.BlockSpec((tile, D_MODEL), lambda i: (i, 0)),
            pl.BlockSpec((D_MODEL, PEER_HEADS * PEER_DKEY), lambda i: (0, 0)),
            pl.BlockSpec((PEER_NKEYS, PEER_DKEY // 2), lambda i: (0, 0)),
            pl.BlockSpec((PEER_NKEYS, PEER_DKEY // 2), lambda i: (0, 0)),
        ],
        out_specs=[
            pl.BlockSpec((PEER_HEADS, PEER_NKEYS, tile), lambda i: (0, 0, i)),
            pl.BlockSpec((PEER_HEADS, PEER_NKEYS, tile), lambda i: (0, 0, i)),
            pl.BlockSpec((4, PEER_HEADS, tile), lambda i: (0, 0, i)),
        ],
        out_shape=[
            jax.ShapeDtypeStruct((PEER_HEADS, PEER_NKEYS, t), F32),
            jax.ShapeDtypeStruct((PEER_HEADS, PEER_NKEYS, t), F32),
            jax.ShapeDtypeStruct((4, PEER_HEADS, t), F32),
        ],
        compiler_params=_cparams("parallel"),
        name="peer_route",
    )(h_bf, wq_bf, k1_bf, k2_bf)


def _layer_norm(x, g, b):
    mu = jnp.mean(x, axis=-1, keepdims=True)
    xc = x - mu
    var = jnp.mean(xc * xc, axis=-1, keepdims=True)
    return xc * lax.rsqrt(var + LN_EPS) * g + b


def _gelu_exact(x):
    return 0.5 * x * (1.0 + lax.erf(x * (2.0 ** -0.5)))


def _peer_expert_kernel(hb_ref, h_ref, s1r_ref, s2_ref, st_ref, u_ref, vt_ref, g_ref, b_ref,
                        o_ref, b_s, hid_s, p_s, acc_s, *, e_tile, t_sub):
    j = pl.program_id(1)
    tt = hb_ref.shape[0]
    blocks = e_tile // PEER_NKEYS

    @pl.when(j == 0)
    def _():
        acc_s[...] = jnp.zeros_like(acc_s)
        for h in range(PEER_HEADS):
            b_s[h] = jnp.exp(s2_ref[h] - st_ref[3, h:h + 1, :])

    nt = (((1,), (1,)), ((), ()))
    hid = lax.dot_general(u_ref[...], hb_ref[...], nt, preferred_element_type=F32)
    hid_s[...] = hid.reshape(blocks, PEER_NKEYS, tt)

    def block_body(bi, carry):
        s1blk = s1r_ref[bi]
        for ts in range(tt // t_sub):
            c0 = ts * t_sub
            w = jnp.zeros((PEER_NKEYS, t_sub), F32)
            for h in range(PEER_HEADS):
                s1row = s1blk[h:h + 1, c0:c0 + t_sub]
                arow = jnp.exp(s1row - st_ref[2, h:h + 1, c0:c0 + t_sub]) * st_ref[1, h:h + 1, c0:c0 + t_sub]
                tau = st_ref[0, h:h + 1, c0:c0 + t_sub]
                s = s2_ref[h, :, c0:c0 + t_sub] + s1row
                w = w + jnp.where(s >= tau, b_s[h, :, c0:c0 + t_sub] * arow, 0.0)
            g = _gelu_exact(hid_s[bi, :, c0:c0 + t_sub])
            p_s[bi, :, c0:c0 + t_sub] = (w * g).astype(BF16)
        return carry

    lax.fori_loop(0, blocks, block_body, 0)
    p = p_s[...].reshape(e_tile, tt)
    acc_s[...] += jnp.dot(vt_ref[...], p, preferred_element_type=F32)

    @pl.when(j == pl.num_programs(1) - 1)
    def _():
        peer = acc_s[...].T
        o_ref[...] = _layer_norm(ALPHA * h_ref[...] + peer, g_ref[...], b_ref[...])


def _peer_experts(h_bf, h, s1r, s2, st, u_bf, vt_bf, ln_g, ln_b, t_tile, e_tile):
    t = h.shape[0]
    n_exp = u_bf.shape[0]
    blocks = e_tile // PEER_NKEYS
    grid = (t // t_tile, n_exp // e_tile)
    kern = functools.partial(_peer_expert_kernel, e_tile=e_tile, t_sub=LANES)
    return pl.pallas_call(
        kern,
        grid=grid,
        in_specs=[
            pl.BlockSpec((t_tile, D_MODEL), lambda i, j: (i, 0)),
            pl.BlockSpec((t_tile, D_MODEL), lambda i, j: (i, 0)),
            pl.BlockSpec((blocks, PEER_HEADS, t_tile), lambda i, j: (j, 0, i)),
            pl.BlockSpec((PEER_HEADS, PEER_NKEYS, t_tile), lambda i, j: (0, 0, i)),
            pl.BlockSpec((4, PEER_HEADS, t_tile), lambda i, j: (0, 0, i)),
            pl.BlockSpec((e_tile, D_MODEL), lambda i, j: (j, 0)),
            pl.BlockSpec((D_MODEL, e_tile), lambda i, j: (0, j)),
            pl.BlockSpec((1, D_MODEL), lambda i, j: (0, 0)),
            pl.BlockSpec((1, D_MODEL), lambda i, j: (0, 0)),
        ],
        out_specs=pl.BlockSpec((t_tile, D_MODEL), lambda i, j: (i, 0)),
        out_shape=jax.ShapeDtypeStruct((t, D_MODEL), F32),
        scratch_shapes=[
            pltpu.VMEM((PEER_HEADS, PEER_NKEYS, t_tile), F32),
            pltpu.VMEM((blocks, PEER_NKEYS, t_tile), F32),
            pltpu.VMEM((blocks, PEER_NKEYS, t_tile), BF16),
            pltpu.VMEM((D_MODEL, t_tile), F32),
        ],
        compiler_params=_cparams("parallel", "arbitrary"),
        name="peer_experts",
    )(h_bf, h, s1r, s2, st, u_bf, vt_bf, ln_g, ln_b)


def _peer_and_ln2(h, wq_bf, k1_bf, k2_bf, u_bf, vt_bf, ln_g, ln_b, route_tile=256, t_tile=512, e_tile=1024):
    h_bf = h.astype(BF16)
    s1, s2, st = _peer_route(h_bf, wq_bf, k1_bf, k2_bf, route_tile)
    s1r = jnp.transpose(s1, (1, 0, 2))
    return _peer_experts(h_bf, h, s1r, s2, st, u_bf, vt_bf, ln_g, ln_b, t_tile, e_tile)


def _matmul_kernel(x_ref, w_ref, o_ref):
    o_ref[...] = jnp.dot(x_ref[...], w_ref[...], preferred_element_type=F32)


def _matmul(x_bf, w_bf, tm, tn):
    m, k = x_bf.shape
    n = w_bf.shape[1]
    return pl.pallas_call(
        _matmul_kernel,
        grid=(m // tm, n // tn),
        in_specs=[pl.BlockSpec((tm, k), lambda i, j: (i, 0)),
                  pl.BlockSpec((k, tn), lambda i, j: (0, j))],
        out_specs=pl.BlockSpec((tm, tn), lambda i, j: (i, j)),
        out_shape=jax.ShapeDtypeStruct((m, n), F32),
        compiler_params=_cparams("parallel", "parallel"),
        name="projection",
    )(x_bf, w_bf)


def _out_proj_ln1_kernel(y_ref, w_ref, x_ref, g_ref, b_ref, o_ref):
    mix = jnp.dot(y_ref[...], w_ref[...], preferred_element_type=F32)
    o_ref[...] = _layer_norm(ALPHA * x_ref[...] + mix, g_ref[...], b_ref[...])


def _out_proj_ln1(y_bf, w_bf, x, g, b, tm):
    m, k = y_bf.shape
    return pl.pallas_call(
        _out_proj_ln1_kernel,
        grid=(m // tm,),
        in_specs=[pl.BlockSpec((tm, k), lambda i: (i, 0)),
                  pl.BlockSpec((k, D_MODEL), lambda i: (0, 0)),
                  pl.BlockSpec((tm, D_MODEL), lambda i: (i, 0)),
                  pl.BlockSpec((1, D_MODEL), lambda i: (0, 0)),
                  pl.BlockSpec((1, D_MODEL), lambda i: (0, 0))],
        out_specs=pl.BlockSpec((tm, D_MODEL), lambda i: (i, 0)),
        out_shape=jax.ShapeDtypeStruct((m, D_MODEL), F32),
        compiler_params=_cparams("parallel"),
        name="out_proj_ln1",
    )(y_bf, w_bf, x, g, b)


def _rope(t, pos):
    half = HEAD_DIM // 2
    inv = ROPE_THETA ** (-jnp.arange(half, dtype=F32) / half)
    ang = pos.astype(F32)[:, None] * inv[None, :]
    cos = jnp.cos(ang)[None, :, None, :]
    sin = jnp.sin(ang)[None, :, None, :]
    t1, t2 = t[..., :half], t[..., half:]
    return jnp.concatenate([t1 * cos - t2 * sin, t2 * cos + t1 * sin], axis=-1)


def _split_proj(proj, pos):
    b, L, _ = proj.shape
    o1 = SSD_WIDTH
    o2 = o1 + CONV_DIM
    o3 = o2 + SSD_HEADS
    z = proj[..., :o1]
    xbc = proj[..., o1:o2]
    dt_raw = proj[..., o2:o3]
    qkv = proj[..., o3:o3 + 3 * ATT_WIDTH].reshape(b, L, 3, ATT_HEADS, HEAD_DIM)
    return z, xbc, dt_raw, _rope(qkv[:, :, 0], pos), _rope(qkv[:, :, 1], pos), qkv[:, :, 2]


def _ssd_scan(x, dt, A, Bm, Cm, h0):
    b, L, H, P = x.shape
    G, N = Bm.shape[2], Bm.shape[3]
    R = H // G
    q = min(SSD_CHUNK, L)
    c = L // q
    x = x.reshape(b, c, q, G, R, P)
    dt = dt.reshape(b, c, q, G, R)
    Bm = Bm.reshape(b, c, q, G, N)
    Cm = Cm.reshape(b, c, q, G, N)
    acs = jnp.cumsum(dt * A.reshape(G, R), axis=2)
    xdt = x * dt[..., None]
    causal = jnp.tril(jnp.ones((q, q), dtype=bool))[:, :, None, None]
    seg = acs[:, :, :, None] - acs[:, :, None, :]
    decay = jnp.exp(jnp.where(causal, seg, -jnp.inf))
    cb = jnp.einsum('bclgn,bcsgn->bclsg', Cm, Bm)
    y_diag = jnp.einsum('bclsgr,bcsgrp->bclgrp', cb[..., None] * decay, xdt)
    decay_to_end = jnp.exp(acs[:, :, -1:] - acs)
    states = jnp.einsum('bclgn,bclgrp->bcgrpn', Bm, xdt * decay_to_end[..., None])
    chunk_decay = jnp.exp(acs[:, :, -1])

    def step(h, inp):
        dec, st = inp
        return dec[..., None, None] * h + st, h

    h_final, h_in = lax.scan(step, h0.reshape(b, G, R, P, N),
                             (jnp.moveaxis(chunk_decay, 1, 0), jnp.moveaxis(states, 1, 0)))
    h_in = jnp.moveaxis(h_in, 0, 1)
    y_off = jnp.einsum('bclgn,bcgrpn->bclgrp', Cm, h_in) * jnp.exp(acs)[..., None]
    y = (y_diag + y_off).reshape(b, c * q, H, P)
    return y, h_final.reshape(b, H, P, N)


def _ssd_mixer(z, xbc, dt_raw, conv_prev, h0, conv_w, conv_b, dt_bias, a_log, d_skip, norm_g):
    b, L, _ = z.shape
    xbc_in = jnp.concatenate([conv_prev, xbc], axis=1)
    conv = conv_b + xbc_in[:, 0:L] * conv_w[0]
    for tap in range(1, SSD_CONV):
        conv = conv + xbc_in[:, tap:tap + L] * conv_w[tap]
    xbc_act = jax.nn.silu(conv)
    new_conv = xbc_in[:, -(SSD_CONV - 1):]
    gn = SSD_GROUPS * SSD_STATE
    xs = xbc_act[..., :SSD_WIDTH].reshape(b, L, SSD_HEADS, HEAD_DIM)
    bm = xbc_act[..., SSD_WIDTH:SSD_WIDTH + gn].reshape(b, L, SSD_GROUPS, SSD_STATE)
    cm = xbc_act[..., SSD_WIDTH + gn:].reshape(b, L, SSD_GROUPS, SSD_STATE)
    dt = jax.nn.softplus(dt_raw + dt_bias)
    A = -jnp.exp(a_log)
    y, h = _ssd_scan(xs, dt, A, bm, cm, h0)
    y = y + xs * d_skip[:, None]
    gated = y.reshape(b, L, SSD_WIDTH) * jax.nn.silu(z)
    gg = gated.reshape(b, L, SSD_GROUPS, SSD_WIDTH // SSD_GROUPS)
    gg = gg * lax.rsqrt(jnp.mean(gg * gg, axis=-1, keepdims=True) + LN_EPS)
    return gg.reshape(b, L, SSD_WIDTH) * norm_g, new_conv, h


def _band_attention(q, k, v, n):
    blk = 128
    b, L, H, hd = q.shape
    nb = L // blk
    qb = q.reshape(b, nb, blk, H, hd)
    kb = k.reshape(b, nb, blk, H, hd)
    vb = v.reshape(b, nb, blk, H, hd)
    kc = jnp.concatenate([jnp.concatenate([jnp.zeros_like(kb[:, :1]), kb[:, :-1]], axis=1), kb], axis=2)
    vc = jnp.concatenate([jnp.concatenate([jnp.zeros_like(vb[:, :1]), vb[:, :-1]], axis=1), vb], axis=2)
    s = jnp.einsum('bnqhd,bnkhd->bnhqk', qb, kc) * (hd ** -0.5)
    qi = jnp.arange(blk)[:, None] + blk
    ki = jnp.arange(2 * blk)[None, :]
    dist = qi - ki
    key_pos = (jnp.arange(nb) * blk)[:, None, None] + ki[None] - blk
    valid = (dist >= 0)[None] & (dist <= n)[None] & (key_pos >= 0)
    s = jnp.where(valid[None, :, None], s, -jnp.inf)
    lse = jax.nn.logsumexp(s, axis=-1)
    p = jnp.exp(s - lse[..., None])
    o = jnp.einsum('bnhqk,bnkhd->bnqhd', p, vc).reshape(b, L, H, hd)
    lse = jnp.transpose(lse, (0, 1, 3, 2)).reshape(b, L, H)
    return o, lse


def _combine_dilations(outs, lses):
    wts = jax.nn.softmax(jnp.stack(lses), axis=0)
    return jnp.einsum('pblh,pblhd->blhd', wts, jnp.stack(outs))


def _attention_prompt(q, k, v):
    b, L, H, hd = q.shape
    outs, lses = [], []
    for w, d in DIL_PATTERNS:
        n = w // d
        to_res = lambda t: jnp.transpose(t.reshape(b, L // d, d, H, hd), (0, 2, 1, 3, 4)).reshape(b * d, L // d, H, hd)
        o, lse = _band_attention(to_res(q), to_res(k), to_res(v), n)
        outs.append(jnp.transpose(o.reshape(b, d, L // d, H, hd), (0, 2, 1, 3, 4)).reshape(b, L, H, hd))
        lses.append(jnp.transpose(lse.reshape(b, d, L // d, H), (0, 2, 1, 3)).reshape(b, L, H))
    return _combine_dilations(outs, lses)


def _attention_sample(q, k_all, v_all):
    Lq = q.shape[1]
    hd = q.shape[-1]
    off = k_all.shape[1] - Lq
    outs, lses = [], []
    for w, d in DIL_PATTERNS:
        n = w // d
        idx = off + jnp.arange(Lq)[:, None] - d * jnp.arange(n + 1)[None, :]
        valid = idx >= 0
        idx = jnp.maximum(idx, 0)
        kg = k_all[:, idx]
        vg = v_all[:, idx]
        s = jnp.einsum('bqhd,bqmhd->bhqm', q, kg) * (hd ** -0.5)
        s = jnp.where(valid[None, None], s, -jnp.inf)
        lse = jax.nn.logsumexp(s, axis=-1)
        p = jnp.exp(s - lse[..., None])
        outs.append(jnp.einsum('bhqm,bqmhd->bqhd', p, vg))
        lses.append(jnp.transpose(lse, (0, 2, 1)))
    return _combine_dilations(outs, lses)


def kernel(x_prompt, x_sample, cache_attn_k, cache_attn_v, state_conv, state_ssm, w_in, conv_w, conv_b, dt_bias, a_log, d_skip, ssd_norm_g, w_out, ln1_g, ln1_b, peer_w_q, peer_keys_1, peer_keys_2, peer_u, peer_v, ln2_g, ln2_b):
    bp, lp, _ = x_prompt.shape
    bs, ls, _ = x_sample.shape
    tp, ts = bp * lp, bs * ls
    in_width = w_in.shape[2]
    n_pad = -in_width % (5 * LANES)
    w_in_bf = jnp.pad(w_in[0], ((0, 0), (0, n_pad))).astype(BF16)
    ssd_w = (conv_w[0], conv_b[0], dt_bias[0], a_log[0], d_skip[0], ssd_norm_g[0])

    x_all = jnp.concatenate([x_prompt.reshape(tp, D_MODEL), x_sample.reshape(ts, D_MODEL)], axis=0)
    proj = _matmul(x_all.astype(BF16), w_in_bf, 512, 5 * LANES)

    z, xbc, dtr, q, k, v = _split_proj(proj[:tp].reshape(bp, lp, -1), jnp.arange(lp))
    conv0 = jnp.zeros((bp, SSD_CONV - 1, CONV_DIM), F32)
    ssm0 = jnp.zeros((bp, SSD_HEADS, HEAD_DIM, SSD_STATE), F32)
    y_p, pconv, pssm = _ssd_mixer(z, xbc, dtr, conv0, ssm0, *ssd_w)
    att_p = _attention_prompt(q, k, v)
    buf_p = min(W_MAX, lp)
    pk, pv = k[:, -buf_p:], v[:, -buf_p:]

    z, xbc, dtr, q, k, v = _split_proj(proj[tp:].reshape(bs, ls, -1), PAST_LEN + jnp.arange(ls))
    y_s, sconv, sssm = _ssd_mixer(z, xbc, dtr, state_conv[0], state_ssm[0], *ssd_w)
    k_all = jnp.concatenate([cache_attn_k[0], k], axis=1)
    v_all = jnp.concatenate([cache_attn_v[0], v], axis=1)
    att_s = _attention_sample(q, k_all, v_all)
    buf_s = cache_attn_k.shape[2]
    sk, sv = k_all[:, -buf_s:], v_all[:, -buf_s:]

    mix_in = jnp.concatenate([
        jnp.concatenate([y_p.reshape(tp, SSD_WIDTH), att_p.reshape(tp, ATT_WIDTH)], axis=-1),
        jnp.concatenate([y_s.reshape(ts, SSD_WIDTH), att_s.reshape(ts, ATT_WIDTH)], axis=-1)], axis=0)
    h = _out_proj_ln1(mix_in.astype(BF16), w_out[0].astype(BF16), x_all, ln1_g, ln1_b, 512)

    out = _peer_and_ln2(h, peer_w_q[0].astype(BF16), peer_keys_1[0].astype(BF16), peer_keys_2[0].astype(BF16),
                        peer_u[0].astype(BF16), peer_v[0].T.astype(BF16), ln2_g, ln2_b)
    y_prompt = out[:tp].reshape(bp, lp, D_MODEL)
    y_sample = out[tp:].reshape(bs, ls, D_MODEL)
    return (y_prompt, y_sample, pk[None], pv[None], pconv[None], pssm[None],
            sk[None], sv[None], sconv[None], sssm[None])
```

```python
import functools
import math

import jax
import jax.numpy as jnp
from jax import lax
from jax.experimental import pallas as pl
from jax.experimental.pallas import tpu as pltpu

F32 = jnp.float32
BF16 = jnp.bfloat16

LANES = 128
SUBLANES = 8
VMEM_LIMIT_BYTES = 56 * 1024 * 1024

D_MODEL = 1024
HEAD_DIM = 64
ATT_HEADS = 8
ATT_WIDTH = ATT_HEADS * HEAD_DIM
SSD_HEADS = 24
SSD_WIDTH = SSD_HEADS * HEAD_DIM
SSD_GROUPS = 4
SSD_STATE = 128
SSD_CONV = 4
SSD_CHUNK = 128
CONV_DIM = SSD_WIDTH + 2 * SSD_GROUPS * SSD_STATE
DIL_PATTERNS = ((128, 1), (512, 4), (2048, 16))
W_MAX = 2048
PAST_LEN = 8192
ROPE_THETA = 10000.0
PEER_NKEYS = 128
PEER_HEADS = 8
PEER_TOPK = 16
PEER_DKEY = 256
ALPHA = 2.0 ** 0.25
LN_EPS = 1e-5

NEG_INF = float("-inf")


def _cparams(*sem):
    return pltpu.CompilerParams(dimension_semantics=sem, vmem_limit_bytes=VMEM_LIMIT_BYTES)


def _compare_exchange(v, i, j):
    hi = jnp.maximum(v[i], v[j])
    lo = jnp.minimum(v[i], v[j])
    v[i], v[j] = hi, lo


def _bitonic_sort_desc(v):
    n = len(v)
    k = 2
    while k <= n:
        j = k // 2
        while j >= 1:
            for i in range(n):
                l = i ^ j
                if l > i:
                    if (i & k) == 0:
                        _compare_exchange(v, i, l)
                    else:
                        _compare_exchange(v, l, i)
            j //= 2
        k *= 2
    return v


def _bitonic_merge_desc(v):
    n = len(v)
    j = n // 2
    while j >= 1:
        for i in range(n):
            if (i & j) == 0:
                _compare_exchange(v, i, i + j)
        j //= 2
    return v


def _top_sorted_union(a, b):
    n = len(a)
    return _bitonic_merge_desc([jnp.maximum(a[i], b[n - 1 - i]) for i in range(n)])


def _top16_over_keys(s):
    t = s.shape[1]
    v = [s[SUBLANES * i:SUBLANES * (i + 1), :] for i in range(PEER_NKEYS // SUBLANES)]
    v = _bitonic_sort_desc(v)
    shift = SUBLANES // 2
    while shift >= 1:
        w = [pltpu.roll(x, SUBLANES - shift, 0) for x in v]
        v = _top_sorted_union(v, w)
        shift //= 2
    del t
    return v


def _peer_route_kernel(h_ref, wq_ref, k1_ref, k2_ref, s1_ref, s2_ref, st_ref):
    tt = h_ref.shape[0]
    half = PEER_DKEY // 2
    qh = jnp.dot(h_ref[...], wq_ref[...], preferred_element_type=F32).astype(BF16)
    row = lax.broadcasted_iota(jnp.int32, (SUBLANES, tt), 0)
    v1 = [jnp.zeros((SUBLANES, tt), F32) for _ in range(PEER_TOPK)]
    v2 = [jnp.zeros((SUBLANES, tt), F32) for _ in range(PEER_TOPK)]
    nt = (((1,), (1,)), ((), ()))
    for h in range(PEER_HEADS):
        q1 = qh[:, h * PEER_DKEY:h * PEER_DKEY + half]
        q2 = qh[:, h * PEER_DKEY + half:(h + 1) * PEER_DKEY]
        s1 = lax.dot_general(k1_ref[...], q1, nt, preferred_element_type=F32)
        s2 = lax.dot_general(k2_ref[...], q2, nt, preferred_element_type=F32)
        s1_ref[h] = s1
        s2_ref[h] = s2
        t1 = _top16_over_keys(s1)
        t2 = _top16_over_keys(s2)
        for a in range(PEER_TOPK):
            r1 = pltpu.roll(t1[a], h, 0) if h else t1[a]
            r2 = pltpu.roll(t2[a], h, 0) if h else t2[a]
            v1[a] = jnp.where(row == h, r1, v1[a])
            v2[a] = jnp.where(row == h, r2, v2[a])
    top = [v1[0] + v2[b] for b in range(PEER_TOPK)]
    neg = jnp.full((SUBLANES, tt), NEG_INF, F32)
    for a in range(1, PEER_TOPK):
        n_a = PEER_TOPK // (a + 1)
        r = [v1[a] + v2[b] for b in range(n_a)] + [neg] * (PEER_TOPK - n_a)
        top = _top_sorted_union(top, r)
    m = top[0]
    z = jnp.zeros_like(m)
    for i in range(PEER_TOPK):
        z = z + jnp.exp(top[i] - m)
    st_ref[0] = top[PEER_TOPK - 1]
    st_ref[1] = 1.0 / z
    st_ref[2] = v1[0]
    st_ref[3] = v2[0]


def _peer_route(h_bf, wq_bf, k1_bf, k2_bf, tile):
    t = h_bf.shape[0]
    grid = (t // tile,)
    return pl.pallas_call(
        _peer_route_kernel,
        grid=grid,
        in_specs=[
            pl.BlockSpec((tile, D_MODEL), lambda i: (i, 0)),
            pl.BlockSpec((D_MODEL, PEER_HEADS * PEER_DKEY), lambda i: (0, 0)),
            pl.BlockSpec((PEER_NKEYS, PEER_DKEY // 2), lambda i: (0, 0)),
            pl.BlockSpec((PEER_NKEYS, PEER_DKEY // 2), lambda i: (0, 0)),
        ],
        out_specs=[
            pl.BlockSpec((PEER_HEADS, PEER_NKEYS, tile), lambda i: (0, 0, i)),
            pl.BlockSpec((PEER_HEADS, PEER_NKEYS, tile), lambda i: (0, 0, i)),
            pl.BlockSpec((4, PEER_HEADS, tile), lambda i: (0, 0, i)),
        ],
        out_shape=[
            jax.ShapeDtypeStruct((PEER_HEADS, PEER_NKEYS, t), F32),
            jax.ShapeDtypeStruct((PEER_HEADS, PEER_NKEYS, t), F32),
            jax.ShapeDtypeStruct((4, PEER_HEADS, t), F32),
        ],
        compiler_params=_cparams("parallel"),
        name="peer_route",
    )(h_bf, wq_bf, k1_bf, k2_bf)


def _layer_norm(x, g, b):
    mu = jnp.mean(x, axis=-1, keepdims=True)
    xc = x - mu
    var = jnp.mean(xc * xc, axis=-1, keepdims=True)
    return xc * lax.rsqrt(var + LN_EPS) * g + b


def _gelu_exact(x):
    return 0.5 * x * (1.0 + lax.erf(x * (2.0 ** -0.5)))


def _peer_expert_kernel(hb_ref, h_ref, s1r_ref, s2_ref, st_ref, u_ref, vt_ref, g_ref, b_ref,
                        o_ref, b_s, hid_s, p_s, acc_s, *, e_tile, t_sub):
    j = pl.program_id(1)
    tt = hb_ref.shape[0]
    blocks = e_tile // PEER_NKEYS

    @pl.when(j == 0)
    def _():
        acc_s[...] = jnp.zeros_like(acc_s)
        for h in range(PEER_HEADS):
            b_s[h] = jnp.exp(s2_ref[h] - st_ref[3, h:h + 1, :])

    nt = (((1,), (1,)), ((), ()))
    hid = lax.dot_general(u_ref[...], hb_ref[...], nt, preferred_element_type=F32)
    hid_s[...] = hid.reshape(blocks, PEER_NKEYS, tt)

    def block_body(bi, carry):
        s1blk = s1r_ref[bi]
        for ts in range(tt // t_sub):
            c0 = ts * t_sub
            w = jnp.zeros((PEER_NKEYS, t_sub), F32)
            for h in range(PEER_HEADS):
                s1row = s1blk[h:h + 1, c0:c0 + t_sub]
                arow = jnp.exp(s1row - st_ref[2, h:h + 1, c0:c0 + t_sub]) * st_ref[1, h:h + 1, c0:c0 + t_sub]
                tau = st_ref[0, h:h + 1, c0:c0 + t_sub]
                s = s2_ref[h, :, c0:c0 + t_sub] + s1row
                w = w + jnp.where(s >= tau, b_s[h, :, c0:c0 + t_sub] * arow, 0.0)
            g = _gelu_exact(hid_s[bi, :, c0:c0 + t_sub])
            p_s[bi, :, c0:c0 + t_sub] = (w * g).astype(BF16)
        return carry

    lax.fori_loop(0, blocks, block_body, 0)
    p = p_s[...].reshape(e_tile, tt)
    acc_s[...] += jnp.dot(vt_ref[...], p, preferred_element_type=F32)

    @pl.when(j == pl.num_programs(1) - 1)
    def _():
        peer = acc_s[...].T
        o_ref[...] = _layer_norm(ALPHA * h_ref[...] + peer, g_ref[...], b_ref[...])


def _peer_experts(h_bf, h, s1r, s2, st, u_bf, vt_bf, ln_g, ln_b, t_tile, e_tile):
    t = h.shape[0]
    n_exp = u_bf.shape[0]
    blocks = e_tile // PEER_NKEYS
    grid = (t // t_tile, n_exp // e_tile)
    kern = functools.partial(_peer_expert_kernel, e_tile=e_tile, t_sub=LANES)
    return pl.pallas_call(
        kern,
        grid=grid,
        in_specs=[
            pl.BlockSpec((t_tile, D_MODEL), lambda i, j: (i, 0)),
            pl.BlockSpec((t_tile, D_MODEL), lambda i, j: (i, 0)),
            pl.BlockSpec((blocks, PEER_HEADS, t_tile), lambda i, j: (j, 0, i)),
            pl.BlockSpec((PEER_HEADS, PEER_NKEYS, t_tile), lambda i, j: (0, 0, i)),
            pl.BlockSpec((4, PEER_HEADS, t_tile), lambda i, j: (0, 0, i)),
            pl.BlockSpec((e_tile, D_MODEL), lambda i, j: (j, 0)),
            pl.BlockSpec((D_MODEL, e_tile), lambda i, j: (0, j)),
            pl.BlockSpec((1, D_MODEL), lambda i, j: (0, 0)),
            pl.BlockSpec((1, D_MODEL), lambda i, j: (0, 0)),
        ],
        out_specs=pl.BlockSpec((t_tile, D_MODEL), lambda i, j: (i, 0)),
        out_shape=jax.ShapeDtypeStruct((t, D_MODEL), F32),
        scratch_shapes=[
            pltpu.VMEM((PEER_HEADS, PEER_NKEYS, t_tile), F32),
            pltpu.VMEM((blocks, PEER_NKEYS, t_tile), F32),
            pltpu.VMEM((blocks, PEER_NKEYS, t_tile), BF16),
            pltpu.VMEM((D_MODEL, t_tile), F32),
        ],
        compiler_params=_cparams("parallel", "arbitrary"),
        name="peer_experts",
    )(h_bf, h, s1r, s2, st, u_bf, vt_bf, ln_g, ln_b)


def _peer_and_ln2(h, wq_bf, k1_bf, k2_bf, u_bf, vt_bf, ln_g, ln_b, route_tile=256, t_tile=512, e_tile=1024):
    h_bf = h.astype(BF16)
    s1, s2, st = _peer_route(h_bf, wq_bf, k1_bf, k2_bf, route_tile)
    s1r = jnp.transpose(s1, (1, 0, 2))
    return _peer_experts(h_bf, h, s1r, s2, st, u_bf, vt_bf, ln_g, ln_b, t_tile, e_tile)


def _matmul_kernel(x_ref, w_ref, o_ref):
    o_ref[...] = jnp.dot(x_ref[...], w_ref[...], preferred_element_type=F32)


def _matmul(x_bf, w_bf, tm, tn):
    m, k = x_bf.shape
    n = w_bf.shape[1]
    return pl.pallas_call(
        _matmul_kernel,
        grid=(m // tm, n // tn),
        in_specs=[pl.BlockSpec((tm, k), lambda i, j: (i, 0)),
                  pl.BlockSpec((k, tn), lambda i, j: (0, j))],
        out_specs=pl.BlockSpec((tm, tn), lambda i, j: (i, j)),
        out_shape=jax.ShapeDtypeStruct((m, n), F32),
        compiler_params=_cparams("parallel", "parallel"),
        name="projection",
    )(x_bf, w_bf)


def _out_proj_ln1_kernel(y_ref, w_ref, x_ref, g_ref, b_ref, o_ref):
    mix = jnp.dot(y_ref[...], w_ref[...], preferred_element_type=F32)
    o_ref[...] = _layer_norm(ALPHA * x_ref[...] + mix, g_ref[...], b_ref[...])


def _out_proj_ln1(y_bf, w_bf, x, g, b, tm):
    m, k = y_bf.shape
    return pl.pallas_call(
        _out_proj_ln1_kernel,
        grid=(m // tm,),
        in_specs=[pl.BlockSpec((tm, k), lambda i: (i, 0)),
                  pl.BlockSpec((k, D_MODEL), lambda i: (0, 0)),
                  pl.BlockSpec((tm, D_MODEL), lambda i: (i, 0)),
                  pl.BlockSpec((1, D_MODEL), lambda i: (0, 0)),
                  pl.BlockSpec((1, D_MODEL), lambda i: (0, 0))],
        out_specs=pl.BlockSpec((tm, D_MODEL), lambda i: (i, 0)),
        out_shape=jax.ShapeDtypeStruct((m, D_MODEL), F32),
        compiler_params=_cparams("parallel"),
        name="out_proj_ln1",
    )(y_bf, w_bf, x, g, b)


def _rope(t, pos):
    half = HEAD_DIM // 2
    inv = ROPE_THETA ** (-jnp.arange(half, dtype=F32) / half)
    ang = pos.astype(F32)[:, None] * inv[None, :]
    cos = jnp.cos(ang)[None, :, None, :]
    sin = jnp.sin(ang)[None, :, None, :]
    t1, t2 = t[..., :half], t[..., half:]
    return jnp.concatenate([t1 * cos - t2 * sin, t2 * cos + t1 * sin], axis=-1)


def _split_proj(proj, pos):
    b, L, _ = proj.shape
    o1 = SSD_WIDTH
    o2 = o1 + CONV_DIM
    o3 = o2 + SSD_HEADS
    z = proj[..., :o1]
    xbc = proj[..., o1:o2]
    dt_raw = proj[..., o2:o3]
    qkv = proj[..., o3:o3 + 3 * ATT_WIDTH].reshape(b, L, 3, ATT_HEADS, HEAD_DIM)
    return z, xbc, dt_raw, _rope(qkv[:, :, 0], pos), _rope(qkv[:, :, 1], pos), qkv[:, :, 2]


def _ssd_scan(x, dt, A, Bm, Cm, h0):
    b, L, H, P = x.shape
    G, N = Bm.shape[2], Bm.shape[3]
    R = H // G
    q = min(SSD_CHUNK, L)
    c = L // q
    x = x.reshape(b, c, q, G, R, P)
    dt = dt.reshape(b, c, q, G, R)
    Bm = Bm.reshape(b, c, q, G, N)
    Cm = Cm.reshape(b, c, q, G, N)
    acs = jnp.cumsum(dt * A.reshape(G, R), axis=2)
    xdt = x * dt[..., None]
    causal = jnp.tril(jnp.ones((q, q), dtype=bool))[:, :, None, None]
    seg = acs[:, :, :, None] - acs[:, :, None, :]
    decay = jnp.exp(jnp.where(causal, seg, -jnp.inf))
    cb = jnp.einsum('bclgn,bcsgn->bclsg', Cm, Bm)
    y_diag = jnp.einsum('bclsgr,bcsgrp->bclgrp', cb[..., None] * decay, xdt)
    decay_to_end = jnp.exp(acs[:, :, -1:] - acs)
    states = jnp.einsum('bclgn,bclgrp->bcgrpn', Bm, xdt * decay_to_end[..., None])
    chunk_decay = jnp.exp(acs[:, :, -1])

    def step(h, inp):
        dec, st = inp
        return dec[..., None, None] * h + st, h

    h_final, h_in = lax.scan(step, h0.reshape(b, G, R, P, N),
                             (jnp.moveaxis(chunk_decay, 1, 0), jnp.moveaxis(states, 1, 0)))
    h_in = jnp.moveaxis(h_in, 0, 1)
    y_off = jnp.einsum('bclgn,bcgrpn->bclgrp', Cm, h_in) * jnp.exp(acs)[..., None]
    y = (y_diag + y_off).reshape(b, c * q, H, P)
    return y, h_final.reshape(b, H, P, N)


def _ssd_mixer(z, xbc, dt_raw, conv_prev, h0, conv_w, conv_b, dt_bias, a_log, d_skip, norm_g):
    b, L, _ = z.shape
    xbc_in = jnp.concatenate([conv_prev, xbc], axis=1)
    conv = conv_b + xbc_in[:, 0:L] * conv_w[0]
    for tap in range(1, SSD_CONV):
        conv = conv + xbc_in[:, tap:tap + L] * conv_w[tap]
    xbc_act = jax.nn.silu(conv)
    new_conv = xbc_in[:, -(SSD_CONV - 1):]
    gn = SSD_GROUPS * SSD_STATE
    xs = xbc_act[..., :SSD_WIDTH].reshape(b, L, SSD_HEADS, HEAD_DIM)
    bm = xbc_act[..., SSD_WIDTH:SSD_WIDTH + gn].reshape(b, L, SSD_GROUPS, SSD_STATE)
    cm = xbc_act[..., SSD_WIDTH + gn:].reshape(b, L, SSD_GROUPS, SSD_STATE)
    dt = jax.nn.softplus(dt_raw + dt_bias)
    A = -jnp.exp(a_log)
    y, h = _ssd_scan(xs, dt, A, bm, cm, h0)
    y = y + xs * d_skip[:, None]
    gated = y.reshape(b, L, SSD_WIDTH) * jax.nn.silu(z)
    gg = gated.reshape(b, L, SSD_GROUPS, SSD_WIDTH // SSD_GROUPS)
    gg = gg * lax.rsqrt(jnp.mean(gg * gg, axis=-1, keepdims=True) + LN_EPS)
    return gg.reshape(b, L, SSD_WIDTH) * norm_g, new_conv, h


def _band_attention(q, k, v, n):
    blk = 128
    b, L, H, hd = q.shape
    nb = L // blk
    qb = q.reshape(b, nb, blk, H, hd)
    kb = k.reshape(b, nb, blk, H, hd)
    vb = v.reshape(b, nb, blk, H, hd)
    kc = jnp.concatenate([jnp.concatenate([jnp.zeros_like(kb[:, :1]), kb[:, :-1]], axis=1), kb], axis=2)
    vc = jnp.concatenate([jnp.concatenate([jnp.zeros_like(vb[:, :1]), vb[:, :-1]], axis=1), vb], axis=2)
    s = jnp.einsum('bnqhd,bnkhd->bnhqk', qb, kc) * (hd ** -0.5)
    qi = jnp.arange(blk)[:, None] + blk
    ki = jnp.arange(2 * blk)[None, :]
    dist = qi - ki
    key_pos = (jnp.arange(nb) * blk)[:, None, None] + ki[None] - blk
    valid = (dist >= 0)[None] & (dist <= n)[None] & (key_pos >= 0)
    s = jnp.where(valid[None, :, None], s, -jnp.inf)
    lse = jax.nn.logsumexp(s, axis=-1)
    p = jnp.exp(s - lse[..., None])
    o = jnp.einsum('bnhqk,bnkhd->bnqhd', p, vc).reshape(b, L, H, hd)
    lse = jnp.transpose(lse, (0, 1, 3, 2)).reshape(b, L, H)
    return o, lse


def _combine_dilations(outs, lses):
    wts = jax.nn.softmax(jnp.stack(lses), axis=0)
    return jnp.einsum('pblh,pblhd->blhd', wts, jnp.stack(outs))


def _attention_prompt(q, k, v):
    b, L, H, hd = q.shape
    outs, lses = [], []
    for w, d in DIL_PATTERNS:
        n = w // d
        to_res = lambda t: jnp.transpose(t.reshape(b, L // d, d, H, hd), (0, 2, 1, 3, 4)).reshape(b * d, L // d, H, hd)
        o, lse = _band_attention(to_res(q), to_res(k), to_res(v), n)
        outs.append(jnp.transpose(o.reshape(b, d, L // d, H, hd), (0, 2, 1, 3, 4)).reshape(b, L, H, hd))
        lses.append(jnp.transpose(lse.reshape(b, d, L // d, H), (0, 2, 1, 3)).reshape(b, L, H))
    return _combine_dilations(outs, lses)


def _attention_sample(q, k_all, v_all):
    Lq = q.shape[1]
    hd = q.shape[-1]
    off = k_all.shape[1] - Lq
    outs, lses = [], []
    for w, d in DIL_PATTERNS:
        n = w // d
        idx = off + jnp.arange(Lq)[:, None] - d * jnp.arange(n + 1)[None, :]
        valid = idx >= 0
        idx = jnp.maximum(idx, 0)
        kg = k_all[:, idx]
        vg = v_all[:, idx]
        s = jnp.einsum('bqhd,bqmhd->bhqm', q, kg) * (hd ** -0.5)
        s = jnp.where(valid[None, None], s, -jnp.inf)
        lse = jax.nn.logsumexp(s, axis=-1)
        p = jnp.exp(s - lse[..., None])
        outs.append(jnp.einsum('bhqm,bqmhd->bqhd', p, vg))
        lses.append(jnp.transpose(lse, (0, 2, 1)))
    return _combine_dilations(outs, lses)


ATT_BLK = 128
ATT_SUPER = 2048


def _attn_prompt_kernel(q_ref, kp_ref, kc_ref, vp_ref, vc_ref, o_ref, kbuf, vbuf, o0, o1, o2, l0, l1, l2):
    o_s = (o0, o1, o2)
    l_s = (l0, l1, l2)
    sb = pl.program_id(2)
    kbuf[0:ATT_SUPER, :] = kp_ref[...]
    kbuf[ATT_SUPER:, :] = kc_ref[...]
    vbuf[0:ATT_SUPER, :] = vp_ref[...]
    vbuf[ATT_SUPER:, :] = vc_ref[...]
    lane = lax.broadcasted_iota(jnp.int32, (1, 2 * HEAD_DIM), 1)
    first_head = lane < HEAD_DIM
    qi = lax.broadcasted_iota(jnp.int32, (ATT_BLK, 2 * ATT_BLK), 0) + ATT_BLK
    ki = lax.broadcasted_iota(jnp.int32, (ATT_BLK, 2 * ATT_BLK), 1)
    band = (qi - ki >= 0) & (qi - ki <= ATT_BLK)
    nt = (((1,), (1,)), ((), ()))
    for p, (_, d) in enumerate(DIL_PATTERNS):
        for u in range(ATT_SUPER // (ATT_BLK * d)):
            valid = band
            if u == 0:
                valid = band & ((ki >= ATT_BLK) | (sb != 0))
            for r in range(d):
                rows = pl.ds(u * ATT_BLK * d + r, ATT_BLK, stride=d)
                krows = pl.ds(ATT_SUPER + (u - 1) * ATT_BLK * d + r, 2 * ATT_BLK, stride=d)
                qs = q_ref[rows, :]
                ks = kbuf[krows, :].astype(BF16)
                vs = vbuf[krows, :].astype(BF16)
                outs, lses = [], []
                for hh in range(2):
                    sel = first_head if hh == 0 else ~first_head
                    qm = jnp.where(sel, qs, 0.0).astype(BF16)
                    s = lax.dot_general(qm, ks, nt, preferred_element_type=F32) * (HEAD_DIM ** -0.5)
                    s = jnp.where(valid, s, NEG_INF)
                    m = jnp.max(s, axis=1, keepdims=True)
                    e = jnp.exp(s - m)
                    l = jnp.sum(e, axis=1, keepdims=True)
                    outs.append(jnp.dot(e.astype(BF16), vs, preferred_element_type=F32) / l)
                    lses.append(m + jnp.log(l))
                o_s[p][rows, :] = jnp.where(first_head, outs[0], outs[1])
                l_s[p][rows, :] = jnp.where(first_head, lses[0], lses[1])
    m = jnp.maximum(jnp.maximum(l_s[0][...], l_s[1][...]), l_s[2][...])
    num = jnp.zeros((ATT_SUPER, 2 * HEAD_DIM), F32)
    den = jnp.zeros((ATT_SUPER, 2 * HEAD_DIM), F32)
    for p in range(len(DIL_PATTERNS)):
        w = jnp.exp(l_s[p][...] - m)
        num = num + w * o_s[p][...]
        den = den + w
    o_ref[...] = num / den


def _attention_prompt_pallas(q, k, v):
    b, L, width = q.shape
    cur = lambda bi, hp, sb: (bi, sb, hp)
    prev = lambda bi, hp, sb: (bi, jnp.maximum(sb - 1, 0), hp)
    blk = (None, ATT_SUPER, 2 * HEAD_DIM)
    return pl.pallas_call(
        _attn_prompt_kernel,
        grid=(b, width // (2 * HEAD_DIM), L // ATT_SUPER),
        in_specs=[pl.BlockSpec(blk, cur), pl.BlockSpec(blk, prev), pl.BlockSpec(blk, cur),
                  pl.BlockSpec(blk, prev), pl.BlockSpec(blk, cur)],
        out_specs=pl.BlockSpec(blk, cur),
        out_shape=jax.ShapeDtypeStruct((b, L, width), F32),
        scratch_shapes=[pltpu.VMEM((2 * ATT_SUPER, 2 * HEAD_DIM), F32), pltpu.VMEM((2 * ATT_SUPER, 2 * HEAD_DIM), F32),
                        *[pltpu.VMEM((ATT_SUPER, 2 * HEAD_DIM), F32) for _ in range(6)]],
        compiler_params=_cparams("parallel", "parallel", "arbitrary"),
        name="dilated_attention_prompt",
    )(q, k, k, v, v)


def kernel(x_prompt, x_sample, cache_attn_k, cache_attn_v, state_conv, state_ssm, w_in, conv_w, conv_b, dt_bias, a_log, d_skip, ssd_norm_g, w_out, ln1_g, ln1_b, peer_w_q, peer_keys_1, peer_keys_2, peer_u, peer_v, ln2_g, ln2_b):
    bp, lp, _ = x_prompt.shape
    bs, ls, _ = x_sample.shape
    tp, ts = bp * lp, bs * ls
    in_width = w_in.shape[2]
    n_pad = -in_width % (5 * LANES)
    w_in_bf = jnp.pad(w_in[0], ((0, 0), (0, n_pad))).astype(BF16)
    ssd_w = (conv_w[0], conv_b[0], dt_bias[0], a_log[0], d_skip[0], ssd_norm_g[0])

    x_all = jnp.concatenate([x_prompt.reshape(tp, D_MODEL), x_sample.reshape(ts, D_MODEL)], axis=0)
    proj = _matmul(x_all.astype(BF16), w_in_bf, 512, 5 * LANES)

    z, xbc, dtr, q, k, v = _split_proj(proj[:tp].reshape(bp, lp, -1), jnp.arange(lp))
    conv0 = jnp.zeros((bp, SSD_CONV - 1, CONV_DIM), F32)
    ssm0 = jnp.zeros((bp, SSD_HEADS, HEAD_DIM, SSD_STATE), F32)
    y_p, pconv, pssm = _ssd_mixer(z, xbc, dtr, conv0, ssm0, *ssd_w)
    att_p = _attention_prompt_pallas(q.reshape(bp, lp, ATT_WIDTH), k.reshape(bp, lp, ATT_WIDTH),
                                     v.reshape(bp, lp, ATT_WIDTH))
    buf_p = min(W_MAX, lp)
    pk, pv = k[:, -buf_p:], v[:, -buf_p:]

    z, xbc, dtr, q, k, v = _split_proj(proj[tp:].reshape(bs, ls, -1), PAST_LEN + jnp.arange(ls))
    y_s, sconv, sssm = _ssd_mixer(z, xbc, dtr, state_conv[0], state_ssm[0], *ssd_w)
    k_all = jnp.concatenate([cache_attn_k[0], k], axis=1)
    v_all = jnp.concatenate([cache_attn_v[0], v], axis=1)
    att_s = _attention_sample(q, k_all, v_all)
    buf_s = cache_attn_k.shape[2]
    sk, sv = k_all[:, -buf_s:], v_all[:, -buf_s:]

    mix_in = jnp.concatenate([
        jnp.concatenate([y_p.reshape(tp, SSD_WIDTH), att_p.reshape(tp, ATT_WIDTH)], axis=-1),
        jnp.concatenate([y_s.reshape(ts, SSD_WIDTH), att_s.reshape(ts, ATT_WIDTH)], axis=-1)], axis=0)
    h = _out_proj_ln1(mix_in.astype(BF16), w_out[0].astype(BF16), x_all, ln1_g, ln1_b, 512)

    out = _peer_and_ln2(h, peer_w_q[0].astype(BF16), peer_keys_1[0].astype(BF16), peer_keys_2[0].astype(BF16),
                        peer_u[0].astype(BF16), peer_v[0].T.astype(BF16), ln2_g, ln2_b)
    y_prompt = out[:tp].reshape(bp, lp, D_MODEL)
    y_sample = out[tp:].reshape(bs, ls, D_MODEL)
    return (y_prompt, y_sample, pk[None], pv[None], pconv[None], pssm[None],
            sk[None], sv[None], sconv[None], sssm[None])
```
